```python
import math
import jax
import jax.numpy as jnp
from jax import lax
import numpy as np


D_MODEL = 4096
BATCH = 2
SEQ = 8192
DEPTH = 4

N_MIXERS = 2
EPS = 1e-6

GDN_K_HEADS = 32
GDN_V_HEADS = 64
GDN_HEAD_K = 128
GDN_HEAD_V = 128
GDN_KEY_DIM = GDN_K_HEADS * GDN_HEAD_K
GDN_VAL_DIM = GDN_V_HEADS * GDN_HEAD_V
GDN_CONV_DIM = 2 * GDN_KEY_DIM + GDN_VAL_DIM
GDN_CONV_K = 4
GDN_CHUNK = 64
GDN_IN_DIM = GDN_CONV_DIM + GDN_VAL_DIM + 2 * GDN_V_HEADS

SGU_WIDTH = 2 * D_MODEL
SGU_GROUPS = 32
SGU_GROUP_DIM = SGU_WIDTH // SGU_GROUPS
SGU_CHUNK = 128
SGU_IN_DIM = 3 * SGU_WIDTH

N_GDN_LAYERS = (DEPTH + N_MIXERS - 1) // N_MIXERS
N_SGU_LAYERS = DEPTH // N_MIXERS

kernel_name = 'hybrid_gdn_sgu_trunk'


def rms_norm(x, w):
    xf = x.astype(jnp.float32)
    y = xf * lax.rsqrt(jnp.mean(xf * xf, axis=-1, keepdims=True) + EPS)
    return (y * w.astype(jnp.float32)).astype(x.dtype)


def layer_norm(x, w, b):
    xf = x.astype(jnp.float32)
    xc = xf - jnp.mean(xf, axis=-1, keepdims=True)
    y = xc * lax.rsqrt(jnp.mean(xc * xc, axis=-1, keepdims=True) + EPS)
    return (y * w.astype(jnp.float32) + b.astype(jnp.float32)).astype(x.dtype)


def l2_normalize(x):
    xf = x.astype(jnp.float32)
    return xf * lax.rsqrt(jnp.sum(xf * xf, axis=-1, keepdims=True) + EPS)


def causal_depthwise_conv(x, w):
    k, c = w.shape
    return lax.conv_general_dilated(
        x, w[:, None, :].astype(x.dtype), window_strides=(1,), padding=[(k - 1, 0)],
        dimension_numbers=('NWC', 'WIO', 'NWC'), feature_group_count=c)


def chunk_gated_delta_rule(q, k, v, g, beta):
    B, T, H, dk = q.shape
    dv = v.shape[-1]
    C = GDN_CHUNK
    N = T // C
    f32 = jnp.float32

    def to_chunks(t):
        return t.astype(f32).reshape(B, N, C, H, -1).transpose(0, 3, 1, 2, 4)

    q, k, v = to_chunks(q), to_chunks(k), to_chunks(v)
    g = g.astype(f32).reshape(B, N, C, H).transpose(0, 3, 1, 2)
    beta = beta.astype(f32).reshape(B, N, C, H).transpose(0, 3, 1, 2)[..., None]
    g = jnp.cumsum(g, axis=-1)

    tril = jnp.tril(jnp.ones((C, C), dtype=bool))
    strict = jnp.tril(jnp.ones((C, C), dtype=bool), -1)
    decay = jnp.exp(jnp.where(tril, g[..., :, None] - g[..., None, :], -jnp.inf))

    k_beta = k * beta
    v_beta = v * beta
    L = jnp.where(strict, jnp.einsum('bhncd,bhnsd->bhncs', k_beta, k) * decay, 0.0)
    A = L + jnp.eye(C, dtype=f32)
    rhs = jnp.concatenate([v_beta, k_beta * jnp.exp(g)[..., None]], axis=-1)
    sol = lax.linalg.triangular_solve(A, rhs, left_side=True, lower=True, unit_diagonal=True)
    u = sol[..., :dv]
    w = sol[..., dv:]

    qk = jnp.where(tril, jnp.einsum('bhncd,bhnsd->bhncs', q, k) * decay, 0.0)
    g_last = g[..., -1]
    q_dec = q * jnp.exp(g)[..., None]
    k_dec = k * jnp.exp(g_last[..., None] - g)[..., None]

    xs = tuple(jnp.moveaxis(t, 2, 0) for t in (q_dec, qk, u, w, k_dec, g_last))

    def step(S, inp):
        q_c, qk_c, u_c, w_c, k_c, gl = inp
        v_new = u_c - jnp.einsum('bhcd,bhde->bhce', w_c, S)
        o = jnp.einsum('bhcd,bhde->bhce', q_c, S) + jnp.einsum('bhcs,bhse->bhce', qk_c, v_new)
        S = S * jnp.exp(gl)[..., None, None] + jnp.einsum('bhcd,bhce->bhde', k_c, v_new)
        return S, o

    S0 = jnp.zeros((B, H, dk, dv), dtype=f32)
    _, o = lax.scan(step, S0, xs)
    return o.transpose(1, 0, 3, 2, 4).reshape(B, T, H, dv)


def gdn_mixer(h, w_in, conv_w, a_log, dt_bias, o_norm_w, w_out):
    B, T, _ = h.shape
    proj = h @ w_in
    qkv, z, b, a = jnp.split(
        proj, [GDN_CONV_DIM, GDN_CONV_DIM + GDN_VAL_DIM, GDN_CONV_DIM + GDN_VAL_DIM + GDN_V_HEADS], axis=-1)
    qkv = jax.nn.silu(causal_depthwise_conv(qkv, conv_w))
    q, k, v = jnp.split(qkv, [GDN_KEY_DIM, 2 * GDN_KEY_DIM], axis=-1)
    q = l2_normalize(q.reshape(B, T, GDN_K_HEADS, GDN_HEAD_K)) * (GDN_HEAD_K ** -0.5)
    k = l2_normalize(k.reshape(B, T, GDN_K_HEADS, GDN_HEAD_K))
    v = v.reshape(B, T, GDN_V_HEADS, GDN_HEAD_V)
    rep = GDN_V_HEADS // GDN_K_HEADS
    q = jnp.repeat(q, rep, axis=2)
    k = jnp.repeat(k, rep, axis=2)
    beta = jax.nn.sigmoid(b.astype(jnp.float32))
    g = -jnp.exp(a_log.astype(jnp.float32)) * jax.nn.softplus(a.astype(jnp.float32) + dt_bias.astype(jnp.float32))
    o = chunk_gated_delta_rule(q, k, v, g, beta).astype(h.dtype)
    o = rms_norm(o, o_norm_w) * jax.nn.silu(z.reshape(B, T, GDN_V_HEADS, GDN_HEAD_V))
    return o.reshape(B, T, GDN_VAL_DIM) @ w_out


def sgu_mixer(h, w_in, ln_w, ln_b, w_s, b_s, w_out):
    B, T, _ = h.shape
    proj = h @ w_in
    u, v, z = jnp.split(proj, 3, axis=-1)
    u = jax.nn.gelu(u, approximate=False)
    v = layer_norm(jax.nn.gelu(v, approximate=False), ln_w, ln_b)
    v = v.reshape(B, T // SGU_CHUNK, SGU_CHUNK, SGU_GROUPS, SGU_GROUP_DIM)
    causal = jnp.tril(jnp.ones((SGU_CHUNK, SGU_CHUNK), dtype=bool))
    ws = jnp.where(causal, w_s, 0.0).astype(v.dtype)
    mixed = jnp.einsum('gts,bnsgd->bntgd', ws, v) + b_s.T[None, None, :, :, None]
    s = u * mixed.reshape(B, T, SGU_WIDTH)
    return (s * jax.nn.silu(z)) @ w_out


def setup_inputs(seed: int = 0) -> dict:
    key = jax.random.key(seed)
    ks = jax.random.split(key, 16)
    f32 = jnp.float32

    def normal(k, shape, scale):
        return jax.random.normal(k, shape, f32) * scale

    x = normal(ks[0], (BATCH, SEQ, D_MODEL), 1.0)
    pre_norm = 1.0 + normal(ks[1], (DEPTH, D_MODEL), 0.02)
    post_norm = 1.0 + normal(ks[2], (DEPTH, D_MODEL), 0.02)
    gdn_w_in = normal(ks[3], (N_GDN_LAYERS, D_MODEL, GDN_IN_DIM), D_MODEL ** -0.5)
    gdn_conv_w = normal(ks[4], (N_GDN_LAYERS, GDN_CONV_K, GDN_CONV_DIM), GDN_CONV_K ** -0.5)
    gdn_a_log = jnp.log(jax.random.uniform(ks[5], (N_GDN_LAYERS, GDN_V_HEADS), f32, 1.0, 16.0))
    dt = jnp.exp(jax.random.uniform(ks[6], (N_GDN_LAYERS, GDN_V_HEADS), f32, math.log(1e-3), math.log(1e-1)))
    gdn_dt_bias = dt + jnp.log(-jnp.expm1(-dt))
    gdn_o_norm = 1.0 + normal(ks[7], (N_GDN_LAYERS, GDN_HEAD_V), 0.02)
    gdn_w_out = normal(ks[8], (N_GDN_LAYERS, GDN_VAL_DIM, D_MODEL), GDN_VAL_DIM ** -0.5)
    sgu_w_in = normal(ks[9], (N_SGU_LAYERS, D_MODEL, SGU_IN_DIM), D_MODEL ** -0.5)
    sgu_ln_w = 1.0 + normal(ks[10], (N_SGU_LAYERS, SGU_WIDTH), 0.02)
    sgu_ln_b = normal(ks[11], (N_SGU_LAYERS, SGU_WIDTH), 0.02)
    sgu_w_s = normal(ks[12], (N_SGU_LAYERS, SGU_GROUPS, SGU_CHUNK, SGU_CHUNK), SGU_CHUNK ** -0.5)
    sgu_b_s = 1.0 + normal(ks[13], (N_SGU_LAYERS, SGU_GROUPS, SGU_CHUNK), 0.02)
    sgu_w_out = normal(ks[14], (N_SGU_LAYERS, SGU_WIDTH, D_MODEL), SGU_WIDTH ** -0.5)
    return {
        'x': x, 'pre_norm': pre_norm, 'post_norm': post_norm,
        'gdn_w_in': gdn_w_in, 'gdn_conv_w': gdn_conv_w, 'gdn_a_log': gdn_a_log,
        'gdn_dt_bias': gdn_dt_bias, 'gdn_o_norm': gdn_o_norm, 'gdn_w_out': gdn_w_out,
        'sgu_w_in': sgu_w_in, 'sgu_ln_w': sgu_ln_w, 'sgu_ln_b': sgu_ln_b,
        'sgu_w_s': sgu_w_s, 'sgu_b_s': sgu_b_s, 'sgu_w_out': sgu_w_out,
    }


def reference(x, pre_norm, post_norm, gdn_w_in, gdn_conv_w, gdn_a_log, gdn_dt_bias, gdn_o_norm,
              gdn_w_out, sgu_w_in, sgu_ln_w, sgu_ln_b, sgu_w_s, sgu_b_s, sgu_w_out):
    for i in range(DEPTH):
        h = rms_norm(x, pre_norm[i])
        j = i // N_MIXERS
        if i % N_MIXERS == 0:
            y = gdn_mixer(h, gdn_w_in[j], gdn_conv_w[j], gdn_a_log[j], gdn_dt_bias[j], gdn_o_norm[j], gdn_w_out[j])
        else:
            y = sgu_mixer(h, sgu_w_in[j], sgu_ln_w[j], sgu_ln_b[j], sgu_w_s[j], sgu_b_s[j], sgu_w_out[j])
        x = x + rms_norm(y, post_norm[i])
    return x
```

```python
import functools
import math

import jax
import jax.numpy as jnp
from jax import lax
from jax.experimental import pallas as pl
from jax.experimental.pallas import tpu as pltpu

EPS = 1e-6
GDN_HEAD = 128
GDN_CHUNK = 64
GDN_HEADS_PER_STEP = 8
GDN_CHUNKS_PER_STEP = 8
VMEM_LIMIT_BYTES = 52 * 1024 * 1024
BF16 = jnp.bfloat16
F32 = jnp.float32


def _params(*semantics):
    return pltpu.CompilerParams(dimension_semantics=semantics, vmem_limit_bytes=VMEM_LIMIT_BYTES)


def _tile(n, want):
    t = want
    while n % t:
        t //= 2
    return t


def _dot(a, b):
    return jnp.dot(a, b, preferred_element_type=F32)


def _dot_nt(a, b):
    return lax.dot_general(a, b, (((1,), (1,)), ((), ())), preferred_element_type=F32)


def _rms(x):
    return x * lax.rsqrt(jnp.mean(x * x, axis=-1, keepdims=True) + EPS)


def _silu(x):
    return x * jax.nn.sigmoid(x)


def _gelu(x):
    return 0.5 * x * (1.0 + lax.erf(x * (1.0 / math.sqrt(2.0))))


def _prenorm_kernel(x_ref, w_ref, h_ref):
    h_ref[...] = (_rms(x_ref[...]) * w_ref[...]).astype(h_ref.dtype)


def _prenorm(x, w):
    m, d = x.shape
    tr = _tile(m, 256)
    return pl.pallas_call(
        _prenorm_kernel,
        grid=(m // tr,),
        in_specs=[pl.BlockSpec((tr, d), lambda i: (i, 0)), pl.BlockSpec((1, d), lambda i: (0, 0))],
        out_specs=pl.BlockSpec((tr, d), lambda i: (i, 0)),
        out_shape=jax.ShapeDtypeStruct((m, d), BF16),
        compiler_params=_params("parallel"),
        name="prenorm",
    )(x, w.reshape(1, d))


def _post_pre_kernel(y_ref, x_ref, pw_ref, nw_ref, xo_ref, h_ref):
    xn = x_ref[...] + _rms(y_ref[...]) * pw_ref[...]
    xo_ref[...] = xn
    h_ref[...] = (_rms(xn) * nw_ref[...]).astype(h_ref.dtype)


def _post_kernel(y_ref, x_ref, pw_ref, xo_ref):
    xo_ref[...] = x_ref[...] + _rms(y_ref[...]) * pw_ref[...]


def _post_pre(y, x, post_w, next_pre_w):
    m, d = x.shape
    tr = _tile(m, 256)
    row = pl.BlockSpec((tr, d), lambda i: (i, 0))
    vec = pl.BlockSpec((1, d), lambda i: (0, 0))
    if next_pre_w is None:
        return pl.pallas_call(
            _post_kernel, grid=(m // tr,), in_specs=[row, row, vec], out_specs=row,
            out_shape=jax.ShapeDtypeStruct((m, d), F32), compiler_params=_params("parallel"),
            name="postnorm",
        )(y, x, post_w.reshape(1, d)), None
    return pl.pallas_call(
        _post_pre_kernel, grid=(m // tr,), in_specs=[row, row, vec, vec], out_specs=[row, row],
        out_shape=[jax.ShapeDtypeStruct((m, d), F32), jax.ShapeDtypeStruct((m, d), BF16)],
        compiler_params=_params("parallel"), name="postnorm_prenorm",
    )(y, x, post_w.reshape(1, d), next_pre_w.reshape(1, d))


def _mm_kernel(a_ref, w_ref, o_ref):
    o_ref[...] = _dot(a_ref[...], w_ref[...]).astype(o_ref.dtype)


def _matmul(a, w, tm, tn, out_dtype, name):
    m, k = a.shape
    n = w.shape[1]
    tm, tn = _tile(m, tm), _tile(n, tn)
    return pl.pallas_call(
        _mm_kernel,
        grid=(n // tn, m // tm),
        in_specs=[pl.BlockSpec((tm, k), lambda j, i: (i, 0)), pl.BlockSpec((k, tn), lambda j, i: (0, j))],
        out_specs=pl.BlockSpec((tm, tn), lambda j, i: (i, j)),
        out_shape=jax.ShapeDtypeStruct((m, n), out_dtype),
        compiler_params=_params("parallel", "parallel"),
        name=name,
    )(a, w)


def _gates_kernel(h_ref, w_ref, alog_ref, dtb_ref, beta_ref, gc_ref):
    hv = alog_ref.shape[-1]
    h = h_ref[...]
    w = w_ref[...]
    w_hi = w.astype(BF16)
    w_lo = (w - w_hi.astype(F32)).astype(BF16)
    ba = _dot(h, w_hi) + _dot(h, w_lo)
    beta_ref[...] = jax.nn.sigmoid(ba[:, :hv])
    t = ba[:, hv:] + dtb_ref[...]
    softplus = jnp.maximum(t, 0.0) + jnp.log1p(jnp.exp(-jnp.abs(t)))
    g = -jnp.exp(alog_ref[...]) * softplus
    tm = h.shape[0]
    r = lax.broadcasted_iota(jnp.int32, (tm, tm), 0)
    c = lax.broadcasted_iota(jnp.int32, (tm, tm), 1)
    shift = GDN_CHUNK.bit_length() - 1
    tri = ((jnp.right_shift(r, shift) == jnp.right_shift(c, shift)) & (c <= r)).astype(F32)
    gc_ref[...] = jnp.dot(tri, g, precision=lax.Precision.HIGHEST, preferred_element_type=F32)


def _gdn_gates(h, w_ba, a_log, dt_bias):
    m, d = h.shape
    hv = a_log.shape[0]
    tm = _tile(m, 512)
    out = jax.ShapeDtypeStruct((m, hv), F32)
    return pl.pallas_call(
        _gates_kernel,
        grid=(m // tm,),
        in_specs=[pl.BlockSpec((tm, d), lambda i: (i, 0)), pl.BlockSpec((d, 2 * hv), lambda i: (0, 0)),
                  pl.BlockSpec((1, hv), lambda i: (0, 0)), pl.BlockSpec((1, hv), lambda i: (0, 0))],
        out_specs=[pl.BlockSpec((tm, hv), lambda i: (i, 0))] * 2,
        out_shape=[out, out],
        compiler_params=_params("parallel"),
        name="gdn_gates",
    )(h, w_ba, a_log.reshape(1, hv), dt_bias.reshape(1, hv))


def _qkv_prep_kernel(halo_ref, x_ref, cw_ref, o_ref, ext_ref, *, n_q_blocks, n_qk_blocks, q_scale):
    ti, j = pl.program_id(1), pl.program_id(2)
    tt = x_ref.shape[0]
    kw = cw_ref.shape[0]
    halo = halo_ref[...].astype(F32)
    ext_ref[0:8, :] = jnp.where(ti == 0, 0.0, halo)
    ext_ref[8:, :] = x_ref[...].astype(F32)
    acc = cw_ref[kw - 1:kw, :] * ext_ref[8:, :]
    for tap in range(kw - 1):
        off = 8 - (kw - 1) + tap
        acc = acc + cw_ref[tap:tap + 1, :] * ext_ref[off:off + tt, :]
    y = _silu(acc)

    @pl.when(j < n_qk_blocks)
    def _():
        scale = jnp.where(j < n_q_blocks, q_scale, 1.0)
        for hh in range(y.shape[1] // GDN_HEAD):
            sl = slice(hh * GDN_HEAD, (hh + 1) * GDN_HEAD)
            blk = y[:, sl]
            inv = lax.rsqrt(jnp.sum(blk * blk, axis=-1, keepdims=True) + EPS) * scale
            o_ref[:, sl] = (blk * inv).astype(o_ref.dtype)

    @pl.when(j >= n_qk_blocks)
    def _():
        o_ref[...] = y.astype(o_ref.dtype)


def _gdn_qkv_prep(proj, conv_w, batch, seq, key_dim):
    kw, conv_dim = conv_w.shape
    assert kw - 1 <= 8
    m = proj.shape[0]
    tt = _tile(seq, 512)
    tc = _tile(key_dim, 1024)
    nt = seq // tt
    kernel = functools.partial(_qkv_prep_kernel, n_q_blocks=key_dim // tc, n_qk_blocks=2 * key_dim // tc,
                               q_scale=GDN_HEAD ** -0.5)
    return pl.pallas_call(
        kernel,
        grid=(batch, nt, conv_dim // tc),
        in_specs=[
            pl.BlockSpec((8, tc), lambda b, t, j: (jnp.maximum((b * nt + t) * (tt // 8) - 1, 0), j)),
            pl.BlockSpec((tt, tc), lambda b, t, j: (b * nt + t, j)),
            pl.BlockSpec((kw, tc), lambda b, t, j: (0, j)),
        ],
        out_specs=pl.BlockSpec((tt, tc), lambda b, t, j: (b * nt + t, j)),
        out_shape=jax.ShapeDtypeStruct((m, conv_dim), BF16),
        scratch_shapes=[pltpu.VMEM((tt + 8, tc), F32)],
        compiler_params=_params("parallel", "parallel", "parallel"),
        name="gdn_qkv_prep",
    )(proj, proj, conv_w)


def _unit_lower_inverse(low, eye):
    inv = eye - low
    power = low
    for _ in range(GDN_CHUNK.bit_length() - 2):
        pb = power.astype(BF16)
        power = _dot(pb, pb)
        inv = inv + _dot(inv.astype(BF16), power.astype(BF16))
    return inv


def _delta_kernel(q_ref, k_ref, v_ref, z_ref, bg_ref, grow_ref, onw_ref, o_ref, s_ref, *, heads, rep, chunks):
    @pl.when(pl.program_id(2) == 0)
    def _():
        s_ref[...] = jnp.zeros_like(s_ref)

    c_len, hd = GDN_CHUNK, GDN_HEAD
    row = lax.broadcasted_iota(jnp.int32, (c_len, c_len), 0)
    col = lax.broadcasted_iota(jnp.int32, (c_len, c_len), 1)
    tril, strict = col <= row, col < row
    eye = (row == col).astype(F32)
    onw = onw_ref[...]

    def chunk(c, carry):
        r0 = pl.multiple_of(c * c_len, c_len)
        rows = pl.ds(r0, c_len)
        for kh in range(heads // rep):
            ksl = slice(kh * hd, (kh + 1) * hd)
            q_c, k_c = q_ref[rows, ksl], k_ref[rows, ksl]
            kk = _dot_nt(k_c, k_c)
            qk = _dot_nt(q_c, k_c)
            qf, kf = q_c.astype(F32), k_c.astype(F32)
            for j in range(rep):
                i = kh * rep + j
                vsl = slice(i * hd, (i + 1) * hd)
                bcol = bg_ref[rows, i:i + 1]
                gcol = bg_ref[rows, heads + i:heads + i + 1]
                grow = grow_ref[c, i:i + 1, :]
                glast = grow[:, c_len - 1:c_len]
                gam = jnp.exp(jnp.where(tril, gcol - grow, -jnp.inf))
                low = jnp.where(strict, kk * bcol * gam, 0.0)
                tinv = _unit_lower_inverse(low, eye)
                eg = jnp.exp(gcol)
                rhs = jnp.concatenate([v_ref[rows, vsl].astype(F32) * bcol, kf * (bcol * eg)], axis=1)
                uw = _dot(tinv.astype(BF16), rhs.astype(BF16))
                u, w = uw[:, :hd], uw[:, hd:]
                state = s_ref[i]
                ws_qs = _dot(jnp.concatenate([w, qf * eg], axis=0).astype(BF16), state.astype(BF16))
                v_new = u - ws_qs[:c_len]
                vb = v_new.astype(BF16)
                o = ws_qs[c_len:] + _dot(jnp.where(tril, qk * gam, 0.0).astype(BF16), vb)
                k_dec = kf * jnp.exp(glast - gcol)
                s_ref[i] = state * jnp.exp(glast) + _dot(k_dec.T.astype(BF16), vb)
                gate = _silu(z_ref[rows, vsl].astype(F32))
                o_ref[rows, vsl] = (_rms(o) * onw * gate).astype(o_ref.dtype)
        return carry

    lax.fori_loop(0, chunks, chunk, 0)


def _gdn_delta(qkv, proj, beta, gcum, o_norm_w, batch, seq, key_dim, val_dim):
    m = qkv.shape[0]
    hd, c_len = GDN_HEAD, GDN_CHUNK
    hk, hv = key_dim // hd, val_dim // hd
    rep = hv // hk
    g = min(GDN_HEADS_PER_STEP, hv)
    assert hv % g == 0 and g % rep == 0
    hg = hv // g
    cb = min(GDN_CHUNKS_PER_STEP, seq // c_len)
    tb = cb * c_len
    assert seq % tb == 0
    nt = seq // tb
    kw_ = g // rep * hd
    vw = g * hd
    bg = jnp.concatenate([beta.reshape(m, hg, g), gcum.reshape(m, hg, g)], axis=-1).transpose(1, 0, 2)
    grow = gcum.reshape(m // c_len, c_len, hv).transpose(0, 2, 1)
    kernel = functools.partial(_delta_kernel, heads=g, rep=rep, chunks=cb)
    return pl.pallas_call(
        kernel,
        grid=(batch, hg, nt),
        in_specs=[
            pl.BlockSpec((tb, kw_), lambda b, h, t: (b * nt + t, h)),
            pl.BlockSpec((tb, kw_), lambda b, h, t: (b * nt + t, key_dim // kw_ + h)),
            pl.BlockSpec((tb, vw), lambda b, h, t: (b * nt + t, 2 * key_dim // vw + h)),
            pl.BlockSpec((tb, vw), lambda b, h, t: (b * nt + t, (2 * key_dim + val_dim) // vw + h)),
            pl.BlockSpec((None, tb, 2 * g), lambda b, h, t: (h, b * nt + t, 0)),
            pl.BlockSpec((cb, g, c_len), lambda b, h, t: (b * nt + t, h, 0)),
            pl.BlockSpec((1, hd), lambda b, h, t: (0, 0)),
        ],
        out_specs=pl.BlockSpec((tb, vw), lambda b, h, t: (b * nt + t, h)),
        out_shape=jax.ShapeDtypeStruct((m, val_dim), BF16),
        scratch_shapes=[pltpu.VMEM((g, hd, hd), F32)],
        compiler_params=_params("parallel", "parallel", "arbitrary"),
        name="gdn_delta",
    )(qkv, qkv, qkv, proj, bg, grow, o_norm_w.reshape(1, hd))


def _gdn_mixer(h, w_in, conv_w, a_log, dt_bias, o_norm_w, w_out, batch, seq):
    conv_dim = conv_w.shape[1]
    hv = a_log.shape[0]
    val_dim = hv * GDN_HEAD
    key_dim = (conv_dim - val_dim) // 2
    n_main = conv_dim + val_dim
    proj = _matmul(h, w_in[:, :n_main].astype(BF16), 1024, 1024, BF16, "gdn_in_proj")
    beta, gcum = _gdn_gates(h, w_in[:, n_main:], a_log, dt_bias)
    qkv = _gdn_qkv_prep(proj, conv_w, batch, seq, key_dim)
    o = _gdn_delta(qkv, proj, beta, gcum, o_norm_w, batch, seq, key_dim, val_dim)
    return _matmul(o, w_out.astype(BF16), 512, 512, F32, "gdn_out_proj")


def _sgu_mix_kernel(u_ref, v_ref, z_ref, lnw_ref, lnb_ref, ws_ref, bst_ref, o_ref, wsm_ref, vn_ref, *, gd):
    groups, ck, _ = ws_ref.shape

    @pl.when(pl.program_id(0) == 0)
    def _():
        row = lax.broadcasted_iota(jnp.int32, (ck, ck), 0)
        col = lax.broadcasted_iota(jnp.int32, (ck, ck), 1)
        for g in range(groups):
            wsm_ref[g] = jnp.where(col <= row, ws_ref[g], 0.0).astype(wsm_ref.dtype)

    v = _gelu(v_ref[...].astype(F32))
    vc = v - jnp.mean(v, axis=-1, keepdims=True)
    inv = lax.rsqrt(jnp.mean(vc * vc, axis=-1, keepdims=True) + EPS)
    vn_ref[...] = (vc * inv * lnw_ref[...] + lnb_ref[...]).astype(vn_ref.dtype)
    for g in range(groups):
        sl = slice(g * gd, (g + 1) * gd)
        mixed = _dot(wsm_ref[g], vn_ref[:, sl]) + bst_ref[:, g:g + 1]
        u = _gelu(u_ref[:, sl].astype(F32))
        o_ref[:, sl] = (u * mixed * _silu(z_ref[:, sl].astype(F32))).astype(o_ref.dtype)


def _sgu_mix(proj, ln_w, ln_b, w_s, b_s):
    m = proj.shape[0]
    width = ln_w.shape[0]
    groups, ck, _ = w_s.shape
    gd = width // groups
    blk = lambda which: pl.BlockSpec((ck, width), lambda i: (i, which))
    vec = pl.BlockSpec((1, width), lambda i: (0, 0))
    return pl.pallas_call(
        functools.partial(_sgu_mix_kernel, gd=gd),
        grid=(m // ck,),
        in_specs=[blk(0), blk(1), blk(2), vec, vec,
                  pl.BlockSpec((groups, ck, ck), lambda i: (0, 0, 0)),
                  pl.BlockSpec((ck, groups), lambda i: (0, 0))],
        out_specs=pl.BlockSpec((ck, width), lambda i: (i, 0)),
        out_shape=jax.ShapeDtypeStruct((m, width), BF16),
        scratch_shapes=[pltpu.VMEM((groups, ck, ck), BF16), pltpu.VMEM((ck, width), BF16)],
        compiler_params=_params("arbitrary"),
        name="sgu_mix",
    )(proj, proj, proj, ln_w.reshape(1, width), ln_b.reshape(1, width), w_s, b_s.T)


def _sgu_mixer(h, w_in, ln_w, ln_b, w_s, b_s, w_out):
    proj = _matmul(h, w_in.astype(BF16), 1024, 1024, BF16, "sgu_in_proj")
    s = _sgu_mix(proj, ln_w, ln_b, w_s, b_s)
    return _matmul(s, w_out.astype(BF16), 512, 512, F32, "sgu_out_proj")


def kernel(x, pre_norm, post_norm, gdn_w_in, gdn_conv_w, gdn_a_log, gdn_dt_bias, gdn_o_norm, gdn_w_out,
           sgu_w_in, sgu_ln_w, sgu_ln_b, sgu_w_s, sgu_b_s, sgu_w_out):
    batch, seq, d = x.shape
    depth = pre_norm.shape[0]
    xf = x.reshape(batch * seq, d)
    h = _prenorm(xf, pre_norm[0])
    for i in range(depth):
        j = i // 2
        if i % 2 == 0:
            y = _gdn_mixer(h, gdn_w_in[j], gdn_conv_w[j], gdn_a_log[j], gdn_dt_bias[j], gdn_o_norm[j],
                           gdn_w_out[j], batch, seq)
        else:
            y = _sgu_mixer(h, sgu_w_in[j], sgu_ln_w[j], sgu_ln_b[j], sgu_w_s[j], sgu_b_s[j], sgu_w_out[j])
        xf, h = _post_pre(y, xf, post_norm[i], pre_norm[i + 1] if i + 1 < depth else None)
    return xf.reshape(batch, seq, d)
```

```python
import functools
import math

import jax
import jax.numpy as jnp
from jax import lax
from jax.experimental import pallas as pl
from jax.experimental.pallas import tpu as pltpu

EPS = 1e-6
GDN_HEAD = 128
GDN_CHUNK = 64
GDN_HEADS_PER_STEP = 8
GDN_CHUNKS_PER_STEP = 8
VMEM_LIMIT_BYTES = 52 * 1024 * 1024
BF16 = jnp.bfloat16
F32 = jnp.float32


def _params(*semantics):
    return pltpu.CompilerParams(dimension_semantics=semantics, vmem_limit_bytes=VMEM_LIMIT_BYTES)


def _tile(n, want):
    t = want
    while n % t:
        t //= 2
    return t


def _dot(a, b):
    return jnp.dot(a, b, preferred_element_type=F32)


def _dot_nt(a, b):
    return lax.dot_general(a, b, (((1,), (1,)), ((), ())), preferred_element_type=F32)


def _rms(x):
    return x * lax.rsqrt(jnp.mean(x * x, axis=-1, keepdims=True) + EPS)


def _silu(x):
    return x * jax.nn.sigmoid(x)


def _gelu(x):
    return 0.5 * x * (1.0 + lax.erf(x * (1.0 / math.sqrt(2.0))))


def _prenorm_kernel(x_ref, w_ref, h_ref):
    h_ref[...] = (_rms(x_ref[...]) * w_ref[...]).astype(h_ref.dtype)


def _prenorm(x, w):
    m, d = x.shape
    tr = _tile(m, 256)
    return pl.pallas_call(
        _prenorm_kernel,
        grid=(m // tr,),
        in_specs=[pl.BlockSpec((tr, d), lambda i: (i, 0)), pl.BlockSpec((1, d), lambda i: (0, 0))],
        out_specs=pl.BlockSpec((tr, d), lambda i: (i, 0)),
        out_shape=jax.ShapeDtypeStruct((m, d), BF16),
        compiler_params=_params("parallel"),
        name="prenorm",
    )(x, w.reshape(1, d))


def _post_pre_kernel(y_ref, x_ref, pw_ref, nw_ref, xo_ref, h_ref):
    xn = x_ref[...] + _rms(y_ref[...]) * pw_ref[...]
    xo_ref[...] = xn
    h_ref[...] = (_rms(xn) * nw_ref[...]).astype(h_ref.dtype)


def _post_kernel(y_ref, x_ref, pw_ref, xo_ref):
    xo_ref[...] = x_ref[...] + _rms(y_ref[...]) * pw_ref[...]


def _post_pre(y, x, post_w, next_pre_w):
    m, d = x.shape
    tr = _tile(m, 256)
    row = pl.BlockSpec((tr, d), lambda i: (i, 0))
    vec = pl.BlockSpec((1, d), lambda i: (0, 0))
    if next_pre_w is None:
        return pl.pallas_call(
            _post_kernel, grid=(m // tr,), in_specs=[row, row, vec], out_specs=row,
            out_shape=jax.ShapeDtypeStruct((m, d), F32), compiler_params=_params("parallel"),
            name="postnorm",
        )(y, x, post_w.reshape(1, d)), None
    return pl.pallas_call(
        _post_pre_kernel, grid=(m // tr,), in_specs=[row, row, vec, vec], out_specs=[row, row],
        out_shape=[jax.ShapeDtypeStruct((m, d), F32), jax.ShapeDtypeStruct((m, d), BF16)],
        compiler_params=_params("parallel"), name="postnorm_prenorm",
    )(y, x, post_w.reshape(1, d), next_pre_w.reshape(1, d))


def _mm_kernel(a_ref, w_ref, o_ref):
    o_ref[...] = _dot(a_ref[...], w_ref[...]).astype(o_ref.dtype)


def _matmul(a, w_stack, layer, n_cols, tm, tn, out_dtype, name):
    m, k = a.shape
    tm, tn = _tile(m, tm), _tile(n_cols, tn)
    return pl.pallas_call(
        _mm_kernel,
        grid=(n_cols // tn, m // tm),
        in_specs=[pl.BlockSpec((tm, k), lambda j, i: (i, 0)),
                  pl.BlockSpec((None, k, tn), lambda j, i: (layer, 0, j))],
        out_specs=pl.BlockSpec((tm, tn), lambda j, i: (i, j)),
        out_shape=jax.ShapeDtypeStruct((m, n_cols), out_dtype),
        compiler_params=_params("parallel", "parallel"),
        name=name,
    )(a, w_stack)


def _gates_kernel(h_ref, w_ref, alog_ref, dtb_ref, beta_ref, gc_ref):
    hv = alog_ref.shape[-1]
    h = h_ref[...]
    w = w_ref[...]
    w_hi = w.astype(BF16)
    w_lo = (w - w_hi.astype(F32)).astype(BF16)
    ba = _dot(h, w_hi) + _dot(h, w_lo)
    beta_ref[...] = jax.nn.sigmoid(ba[:, :hv])
    t = ba[:, hv:] + dtb_ref[...]
    softplus = jnp.maximum(t, 0.0) + jnp.log1p(jnp.exp(-jnp.abs(t)))
    g = -jnp.exp(alog_ref[...]) * softplus
    tm = h.shape[0]
    r = lax.broadcasted_iota(jnp.int32, (tm, tm), 0)
    c = lax.broadcasted_iota(jnp.int32, (tm, tm), 1)
    shift = GDN_CHUNK.bit_length() - 1
    tri = ((jnp.right_shift(r, shift) == jnp.right_shift(c, shift)) & (c <= r)).astype(F32)
    gc_ref[...] = jnp.dot(tri, g, precision=lax.Precision.HIGHEST, preferred_element_type=F32)


def _gdn_gates(h, w_in_stack, layer, col0, a_log, dt_bias):
    m, d = h.shape
    hv = a_log.shape[0]
    assert col0 % (2 * hv) == 0
    tm = _tile(m, 512)
    out = jax.ShapeDtypeStruct((m, hv), F32)
    return pl.pallas_call(
        _gates_kernel,
        grid=(m // tm,),
        in_specs=[pl.BlockSpec((tm, d), lambda i: (i, 0)),
                  pl.BlockSpec((None, d, 2 * hv), lambda i: (layer, 0, col0 // (2 * hv))),
                  pl.BlockSpec((1, hv), lambda i: (0, 0)), pl.BlockSpec((1, hv), lambda i: (0, 0))],
        out_specs=[pl.BlockSpec((tm, hv), lambda i: (i, 0))] * 2,
        out_shape=[out, out],
        compiler_params=_params("parallel"),
        name="gdn_gates",
    )(h, w_in_stack, a_log.reshape(1, hv), dt_bias.reshape(1, hv))


def _qkv_prep_kernel(halo_ref, x_ref, cw_ref, o_ref, ext_ref, *, n_q_blocks, n_qk_blocks, q_scale):
    ti, j = pl.program_id(1), pl.program_id(2)
    tt = x_ref.shape[0]
    kw = cw_ref.shape[0]
    halo = halo_ref[...].astype(F32)
    ext_ref[0:8, :] = jnp.where(ti == 0, 0.0, halo)
    ext_ref[8:, :] = x_ref[...].astype(F32)
    acc = cw_ref[kw - 1:kw, :] * ext_ref[8:, :]
    for tap in range(kw - 1):
        off = 8 - (kw - 1) + tap
        acc = acc + cw_ref[tap:tap + 1, :] * ext_ref[off:off + tt, :]
    y = _silu(acc)

    @pl.when(j < n_qk_blocks)
    def _():
        scale = jnp.where(j < n_q_blocks, q_scale, 1.0)
        for hh in range(y.shape[1] // GDN_HEAD):
            sl = slice(hh * GDN_HEAD, (hh + 1) * GDN_HEAD)
            blk = y[:, sl]
            inv = lax.rsqrt(jnp.sum(blk * blk, axis=-1, keepdims=True) + EPS) * scale
            o_ref[:, sl] = (blk * inv).astype(o_ref.dtype)

    @pl.when(j >= n_qk_blocks)
    def _():
        o_ref[...] = y.astype(o_ref.dtype)


def _gdn_qkv_prep(proj, conv_w, batch, seq, key_dim):
    kw, conv_dim = conv_w.shape
    assert kw - 1 <= 8
    m = proj.shape[0]
    tt = _tile(seq, 512)
    tc = _tile(key_dim, 1024)
    nt = seq // tt
    kernel = functools.partial(_qkv_prep_kernel, n_q_blocks=key_dim // tc, n_qk_blocks=2 * key_dim // tc,
                               q_scale=GDN_HEAD ** -0.5)
    return pl.pallas_call(
        kernel,
        grid=(batch, nt, conv_dim // tc),
        in_specs=[
            pl.BlockSpec((8, tc), lambda b, t, j: (jnp.maximum((b * nt + t) * (tt // 8) - 1, 0), j)),
            pl.BlockSpec((tt, tc), lambda b, t, j: (b * nt + t, j)),
            pl.BlockSpec((kw, tc), lambda b, t, j: (0, j)),
        ],
        out_specs=pl.BlockSpec((tt, tc), lambda b, t, j: (b * nt + t, j)),
        out_shape=jax.ShapeDtypeStruct((m, conv_dim), BF16),
        scratch_shapes=[pltpu.VMEM((tt + 8, tc), F32)],
        compiler_params=_params("parallel", "parallel", "parallel"),
        name="gdn_qkv_prep",
    )(proj, proj, conv_w)


def _delta_kernel(q_ref, k_ref, v_ref, z_ref, bg_ref, grow_ref, onw_ref, o_ref,
                  s_ref, u_ref, wq_ref, qkkd_ref, *, heads, rep, chunks):
    @pl.when(pl.program_id(2) == 0)
    def _():
        s_ref[...] = jnp.zeros_like(s_ref)

    c_len, hd = GDN_CHUNK, GDN_HEAD
    row = lax.broadcasted_iota(jnp.int32, (c_len, c_len), 0)
    col = lax.broadcasted_iota(jnp.int32, (c_len, c_len), 1)
    tril, strict = col <= row, col < row
    eye = (row == col).astype(F32)
    onw = onw_ref[...]
    head_ids = range(heads)
    n_merged = c_len.bit_length() - 3

    def prepare(c, carry):
        rows = pl.ds(pl.multiple_of(c * c_len, c_len), c_len)
        kk, qk, kf, qf = [], [], [], []
        for kh in range(heads // rep):
            ksl = slice(kh * hd, (kh + 1) * hd)
            q_c, k_c = q_ref[rows, ksl], k_ref[rows, ksl]
            both = _dot_nt(jnp.concatenate([k_c, q_c], axis=0), k_c)
            kk += [both[:c_len]] * rep
            qk += [both[c_len:]] * rep
            kf += [k_c.astype(F32)] * rep
            qf += [q_c.astype(F32)] * rep
        bcol = [jnp.broadcast_to(bg_ref[rows, i:i + 1], (c_len, hd)) for i in head_ids]
        gcol = [jnp.broadcast_to(bg_ref[rows, heads + i:heads + i + 1], (c_len, hd)) for i in head_ids]
        grow = [grow_ref[c, i:i + 1, :] for i in head_ids]
        gam = [jnp.exp(jnp.where(tril, gcol[i][:, :c_len] - grow[i], -jnp.inf)) for i in head_ids]
        low = [jnp.where(strict, kk[i] * bcol[i][:, :c_len] * gam[i], 0.0) for i in head_ids]
        lb = [x.astype(BF16) for x in low]
        power = [_dot(lb[i], lb[i]) for i in head_ids]
        inv = [eye - low[i] for i in head_ids]
        for _ in range(n_merged):
            pb = [x.astype(BF16) for x in power]
            both = [_dot(jnp.concatenate([inv[i].astype(BF16), pb[i]], axis=0), pb[i]) for i in head_ids]
            inv = [inv[i] + both[i][:c_len] for i in head_ids]
            power = [both[i][c_len:] for i in head_ids]
        inv = [inv[i] + _dot(inv[i].astype(BF16), power[i].astype(BF16)) for i in head_ids]
        eg = [jnp.exp(gcol[i]) for i in head_ids]
        for i in head_ids:
            vsl = slice(i * hd, (i + 1) * hd)
            rhs = jnp.concatenate([v_ref[rows, vsl].astype(F32) * bcol[i], kf[i] * (bcol[i] * eg[i])], axis=1)
            uw = _dot(inv[i].astype(BF16), rhs.astype(BF16))
            u_ref[c, i] = uw[:, :hd]
            wq_ref[c, i] = jnp.concatenate([uw[:, hd:], qf[i] * eg[i]], axis=0).astype(BF16)
            k_dec = kf[i] * jnp.exp(grow[i][:, c_len - 1:c_len] - gcol[i])
            qkm = jnp.where(tril, qk[i] * gam[i], 0.0)
            qkkd_ref[c, i] = jnp.concatenate([qkm, k_dec.T], axis=0).astype(BF16)
        return carry

    lax.fori_loop(0, chunks, prepare, 0, unroll=2)

    def advance(c, carry):
        rows = pl.ds(pl.multiple_of(c * c_len, c_len), c_len)
        state = [s_ref[i] for i in head_ids]
        ws_qs = [_dot(wq_ref[c, i], state[i].astype(BF16)) for i in head_ids]
        vb = [(u_ref[c, i] - ws_qs[i][:c_len]).astype(BF16) for i in head_ids]
        upd = [_dot(qkkd_ref[c, i], vb[i]) for i in head_ids]
        for i in head_ids:
            vsl = slice(i * hd, (i + 1) * hd)
            decay = jnp.exp(grow_ref[c, i:i + 1, c_len - 1:c_len])
            s_ref[i] = state[i] * decay + upd[i][c_len:]
            o = ws_qs[i][c_len:] + upd[i][:c_len]
            gate = _silu(z_ref[rows, vsl].astype(F32))
            o_ref[rows, vsl] = (_rms(o) * onw * gate).astype(o_ref.dtype)
        return carry

    lax.fori_loop(0, chunks, advance, 0)


def _gdn_delta(qkv, proj, beta, gcum, o_norm_w, batch, seq, key_dim, val_dim):
    m = qkv.shape[0]
    hd, c_len = GDN_HEAD, GDN_CHUNK
    hk, hv = key_dim // hd, val_dim // hd
    rep = hv // hk
    g = min(GDN_HEADS_PER_STEP, hv)
    assert hv % g == 0 and g % rep == 0
    hg = hv // g
    cb = min(GDN_CHUNKS_PER_STEP, seq // c_len)
    tb = cb * c_len
    assert seq % tb == 0
    nt = seq // tb
    kw_ = g // rep * hd
    vw = g * hd
    bg = jnp.concatenate([beta.reshape(m, hg, g), gcum.reshape(m, hg, g)], axis=-1).transpose(1, 0, 2)
    grow = gcum.reshape(m // c_len, c_len, hv).transpose(0, 2, 1)
    kernel = functools.partial(_delta_kernel, heads=g, rep=rep, chunks=cb)
    return pl.pallas_call(
        kernel,
        grid=(batch, hg, nt),
        in_specs=[
            pl.BlockSpec((tb, kw_), lambda b, h, t: (b * nt + t, h)),
            pl.BlockSpec((tb, kw_), lambda b, h, t: (b * nt + t, key_dim // kw_ + h)),
            pl.BlockSpec((tb, vw), lambda b, h, t: (b * nt + t, 2 * key_dim // vw + h)),
            pl.BlockSpec((tb, vw), lambda b, h, t: (b * nt + t, (2 * key_dim + val_dim) // vw + h)),
            pl.BlockSpec((None, tb, 2 * g), lambda b, h, t: (h, b * nt + t, 0)),
            pl.BlockSpec((cb, g, c_len), lambda b, h, t: (b * nt + t, h, 0)),
            pl.BlockSpec((1, hd), lambda b, h, t: (0, 0)),
        ],
        out_specs=pl.BlockSpec((tb, vw), lambda b, h, t: (b * nt + t, h)),
        out_shape=jax.ShapeDtypeStruct((m, val_dim), BF16),
        scratch_shapes=[pltpu.VMEM((g, hd, hd), F32),
                        pltpu.VMEM((cb, g, c_len, hd), F32),
                        pltpu.VMEM((cb, g, 2 * c_len, hd), BF16),
                        pltpu.VMEM((cb, g, c_len + hd, c_len), BF16)],
        compiler_params=_params("parallel", "parallel", "arbitrary"),
        name="gdn_delta",
    )(qkv, qkv, qkv, proj, bg, grow, o_norm_w.reshape(1, hd))


def _gdn_mixer(h, layer, w_in, w_in_bf, conv_w, a_log, dt_bias, o_norm_w, w_out_bf, batch, seq):
    conv_dim = conv_w.shape[1]
    hv = a_log.shape[0]
    val_dim = hv * GDN_HEAD
    key_dim = (conv_dim - val_dim) // 2
    n_main = conv_dim + val_dim
    proj = _matmul(h, w_in_bf, layer, n_main, 1024, 1024, BF16, "gdn_in_proj")
    beta, gcum = _gdn_gates(h, w_in, layer, n_main, a_log, dt_bias)
    qkv = _gdn_qkv_prep(proj, conv_w, batch, seq, key_dim)
    o = _gdn_delta(qkv, proj, beta, gcum, o_norm_w, batch, seq, key_dim, val_dim)
    return _matmul(o, w_out_bf, layer, w_out_bf.shape[-1], 512, 512, F32, "gdn_out_proj")


def _sgu_mix_kernel(u_ref, v_ref, z_ref, lnw_ref, lnb_ref, ws_ref, bst_ref, o_ref, wsm_ref, vn_ref, *, gd):
    groups, ck, _ = ws_ref.shape

    @pl.when(pl.program_id(0) == 0)
    def _():
        row = lax.broadcasted_iota(jnp.int32, (ck, ck), 0)
        col = lax.broadcasted_iota(jnp.int32, (ck, ck), 1)
        for g in range(groups):
            wsm_ref[g] = jnp.where(col <= row, ws_ref[g], 0.0).astype(wsm_ref.dtype)

    v = _gelu(v_ref[...].astype(F32))
    vc = v - jnp.mean(v, axis=-1, keepdims=True)
    inv = lax.rsqrt(jnp.mean(vc * vc, axis=-1, keepdims=True) + EPS)
    vn_ref[...] = (vc * inv * lnw_ref[...] + lnb_ref[...]).astype(vn_ref.dtype)
    for g in range(groups):
        sl = slice(g * gd, (g + 1) * gd)
        mixed = _dot(wsm_ref[g], vn_ref[:, sl]) + bst_ref[:, g:g + 1]
        u = _gelu(u_ref[:, sl].astype(F32))
        o_ref[:, sl] = (u * mixed * _silu(z_ref[:, sl].astype(F32))).astype(o_ref.dtype)


def _sgu_mix(proj, ln_w, ln_b, w_s, b_s):
    m = proj.shape[0]
    width = ln_w.shape[0]
    groups, ck, _ = w_s.shape
    gd = width // groups
    blk = lambda which: pl.BlockSpec((ck, width), lambda i: (i, which))
    vec = pl.BlockSpec((1, width), lambda i: (0, 0))
    return pl.pallas_call(
        functools.partial(_sgu_mix_kernel, gd=gd),
        grid=(m // ck,),
        in_specs=[blk(0), blk(1), blk(2), vec, vec,
                  pl.BlockSpec((groups, ck, ck), lambda i: (0, 0, 0)),
                  pl.BlockSpec((ck, groups), lambda i: (0, 0))],
        out_specs=pl.BlockSpec((ck, width), lambda i: (i, 0)),
        out_shape=jax.ShapeDtypeStruct((m, width), BF16),
        scratch_shapes=[pltpu.VMEM((groups, ck, ck), BF16), pltpu.VMEM((ck, width), BF16)],
        compiler_params=_params("arbitrary"),
        name="sgu_mix",
    )(proj, proj, proj, ln_w.reshape(1, width), ln_b.reshape(1, width), w_s, b_s.T)


def _sgu_mixer(h, layer, w_in_bf, ln_w, ln_b, w_s, b_s, w_out_bf):
    proj = _matmul(h, w_in_bf, layer, w_in_bf.shape[-1], 1024, 1024, BF16, "sgu_in_proj")
    s = _sgu_mix(proj, ln_w, ln_b, w_s, b_s)
    return _matmul(s, w_out_bf, layer, w_out_bf.shape[-1], 512, 512, F32, "sgu_out_proj")


def kernel(x, pre_norm, post_norm, gdn_w_in, gdn_conv_w, gdn_a_log, gdn_dt_bias, gdn_o_norm, gdn_w_out,
           sgu_w_in, sgu_ln_w, sgu_ln_b, sgu_w_s, sgu_b_s, sgu_w_out):
    batch, seq, d = x.shape
    depth = pre_norm.shape[0]
    xf = x.reshape(batch * seq, d)
    h = _prenorm(xf, pre_norm[0])
    gdn_w_in_bf, gdn_w_out_bf = gdn_w_in.astype(BF16), gdn_w_out.astype(BF16)
    sgu_w_in_bf, sgu_w_out_bf = sgu_w_in.astype(BF16), sgu_w_out.astype(BF16)
    for i in range(depth):
        j = i // 2
        if i % 2 == 0:
            y = _gdn_mixer(h, j, gdn_w_in, gdn_w_in_bf, gdn_conv_w[j], gdn_a_log[j], gdn_dt_bias[j],
                           gdn_o_norm[j], gdn_w_out_bf, batch, seq)
        else:
            y = _sgu_mixer(h, j, sgu_w_in_bf, sgu_ln_w[j], sgu_ln_b[j], sgu_w_s[j], sgu_b_s[j], sgu_w_out_bf)
        xf, h = _post_pre(y, xf, post_norm[i], pre_norm[i + 1] if i + 1 < depth else None)
    return xf.reshape(batch, seq, d)
```

```python
import functools
import math

import jax
import jax.numpy as jnp
from jax import lax
from jax.experimental import pallas as pl
from jax.experimental.pallas import tpu as pltpu

EPS = 1e-6
GDN_HEAD = 128
GDN_CHUNK = 64
GDN_HEADS_PER_STEP = 16
GDN_CHUNKS_PER_STEP = 16
VMEM_LIMIT_BYTES = 52 * 1024 * 1024
BF16 = jnp.bfloat16
F32 = jnp.float32


def _params(*semantics):
    return pltpu.CompilerParams(dimension_semantics=semantics, vmem_limit_bytes=VMEM_LIMIT_BYTES)


def _tile(n, want):
    t = want
    while n % t:
        t //= 2
    return t


def _dot(a, b):
    return jnp.dot(a, b, preferred_element_type=F32)


def _dot_nt(a, b):
    return lax.dot_general(a, b, (((1,), (1,)), ((), ())), preferred_element_type=F32)


def _rms(x):
    return x * lax.rsqrt(jnp.mean(x * x, axis=-1, keepdims=True) + EPS)


def _silu(x):
    return x * jax.nn.sigmoid(x)


def _gelu(x):
    return 0.5 * x * (1.0 + lax.erf(x * (1.0 / math.sqrt(2.0))))


def _prenorm_kernel(x_ref, w_ref, h_ref):
    h_ref[...] = (_rms(x_ref[...]) * w_ref[...]).astype(h_ref.dtype)


def _prenorm(x, w):
    m, d = x.shape
    tr = _tile(m, 256)
    return pl.pallas_call(
        _prenorm_kernel,
        grid=(m // tr,),
        in_specs=[pl.BlockSpec((tr, d), lambda i: (i, 0)), pl.BlockSpec((1, d), lambda i: (0, 0))],
        out_specs=pl.BlockSpec((tr, d), lambda i: (i, 0)),
        out_shape=jax.ShapeDtypeStruct((m, d), BF16),
        compiler_params=_params("parallel"),
        name="prenorm",
    )(x, w.reshape(1, d))


def _post_pre_kernel(y_ref, x_ref, pw_ref, nw_ref, xo_ref, h_ref):
    xn = x_ref[...] + _rms(y_ref[...]) * pw_ref[...]
    xo_ref[...] = xn
    h_ref[...] = (_rms(xn) * nw_ref[...]).astype(h_ref.dtype)


def _post_kernel(y_ref, x_ref, pw_ref, xo_ref):
    xo_ref[...] = x_ref[...] + _rms(y_ref[...]) * pw_ref[...]


def _post_pre(y, x, post_w, next_pre_w):
    m, d = x.shape
    tr = _tile(m, 256)
    row = pl.BlockSpec((tr, d), lambda i: (i, 0))
    vec = pl.BlockSpec((1, d), lambda i: (0, 0))
    if next_pre_w is None:
        return pl.pallas_call(
            _post_kernel, grid=(m // tr,), in_specs=[row, row, vec], out_specs=row,
            out_shape=jax.ShapeDtypeStruct((m, d), F32), compiler_params=_params("parallel"),
            name="postnorm",
        )(y, x, post_w.reshape(1, d)), None
    return pl.pallas_call(
        _post_pre_kernel, grid=(m // tr,), in_specs=[row, row, vec, vec], out_specs=[row, row],
        out_shape=[jax.ShapeDtypeStruct((m, d), F32), jax.ShapeDtypeStruct((m, d), BF16)],
        compiler_params=_params("parallel"), name="postnorm_prenorm",
    )(y, x, post_w.reshape(1, d), next_pre_w.reshape(1, d))


def _mm_kernel(a_ref, w_ref, o_ref):
    o_ref[...] = _dot(a_ref[...], w_ref[...]).astype(o_ref.dtype)


def _matmul(a, w_stack, layer, n_cols, tm, tn, out_dtype, name):
    m, k = a.shape
    tm, tn = _tile(m, tm), _tile(n_cols, tn)
    return pl.pallas_call(
        _mm_kernel,
        grid=(n_cols // tn, m // tm),
        in_specs=[pl.BlockSpec((tm, k), lambda j, i: (i, 0)),
                  pl.BlockSpec((None, k, tn), lambda j, i: (layer, 0, j))],
        out_specs=pl.BlockSpec((tm, tn), lambda j, i: (i, j)),
        out_shape=jax.ShapeDtypeStruct((m, n_cols), out_dtype),
        compiler_params=_params("parallel", "parallel"),
        name=name,
    )(a, w_stack)


def _gates_kernel(h_ref, w_ref, alog_ref, dtb_ref, beta_ref, gc_ref):
    hv = alog_ref.shape[-1]
    h = h_ref[...]
    w = w_ref[...]
    w_hi = w.astype(BF16)
    w_lo = (w - w_hi.astype(F32)).astype(BF16)
    ba = _dot(h, w_hi) + _dot(h, w_lo)
    beta_ref[...] = jax.nn.sigmoid(ba[:, :hv])
    t = ba[:, hv:] + dtb_ref[...]
    softplus = jnp.maximum(t, 0.0) + jnp.log1p(jnp.exp(-jnp.abs(t)))
    g = -jnp.exp(alog_ref[...]) * softplus
    tm = h.shape[0]
    r = lax.broadcasted_iota(jnp.int32, (tm, tm), 0)
    c = lax.broadcasted_iota(jnp.int32, (tm, tm), 1)
    shift = GDN_CHUNK.bit_length() - 1
    tri = ((jnp.right_shift(r, shift) == jnp.right_shift(c, shift)) & (c <= r)).astype(F32)
    gc_ref[...] = jnp.dot(tri, g, precision=lax.Precision.HIGHEST, preferred_element_type=F32)


def _gdn_gates(h, w_in_stack, layer, col0, a_log, dt_bias):
    m, d = h.shape
    hv = a_log.shape[0]
    assert col0 % (2 * hv) == 0
    tm = _tile(m, 512)
    out = jax.ShapeDtypeStruct((m, hv), F32)
    return pl.pallas_call(
        _gates_kernel,
        grid=(m // tm,),
        in_specs=[pl.BlockSpec((tm, d), lambda i: (i, 0)),
                  pl.BlockSpec((None, d, 2 * hv), lambda i: (layer, 0, col0 // (2 * hv))),
                  pl.BlockSpec((1, hv), lambda i: (0, 0)), pl.BlockSpec((1, hv), lambda i: (0, 0))],
        out_specs=[pl.BlockSpec((tm, hv), lambda i: (i, 0))] * 2,
        out_shape=[out, out],
        compiler_params=_params("parallel"),
        name="gdn_gates",
    )(h, w_in_stack, a_log.reshape(1, hv), dt_bias.reshape(1, hv))


def _qkv_prep_kernel(halo_ref, x_ref, cw_ref, o_ref, ext_ref, *, n_q_blocks, n_qk_blocks, q_scale):
    ti, j = pl.program_id(1), pl.program_id(2)
    tt = x_ref.shape[0]
    kw = cw_ref.shape[0]
    halo = halo_ref[...].astype(F32)
    ext_ref[0:8, :] = jnp.where(ti == 0, 0.0, halo)
    ext_ref[8:, :] = x_ref[...].astype(F32)
    acc = cw_ref[kw - 1:kw, :] * ext_ref[8:, :]
    for tap in range(kw - 1):
        off = 8 - (kw - 1) + tap
        acc = acc + cw_ref[tap:tap + 1, :] * ext_ref[off:off + tt, :]
    y = _silu(acc)

    @pl.when(j < n_qk_blocks)
    def _():
        scale = jnp.where(j < n_q_blocks, q_scale, 1.0)
        for hh in range(y.shape[1] // GDN_HEAD):
            sl = slice(hh * GDN_HEAD, (hh + 1) * GDN_HEAD)
            blk = y[:, sl]
            inv = lax.rsqrt(jnp.sum(blk * blk, axis=-1, keepdims=True) + EPS) * scale
            o_ref[:, sl] = (blk * inv).astype(o_ref.dtype)

    @pl.when(j >= n_qk_blocks)
    def _():
        o_ref[...] = y.astype(o_ref.dtype)


def _gdn_qkv_prep(proj, conv_w, batch, seq, key_dim):
    kw, conv_dim = conv_w.shape
    assert kw - 1 <= 8
    m = proj.shape[0]
    tt = _tile(seq, 512)
    tc = _tile(key_dim, 1024)
    nt = seq // tt
    kernel = functools.partial(_qkv_prep_kernel, n_q_blocks=key_dim // tc, n_qk_blocks=2 * key_dim // tc,
                               q_scale=GDN_HEAD ** -0.5)
    return pl.pallas_call(
        kernel,
        grid=(batch, nt, conv_dim // tc),
        in_specs=[
            pl.BlockSpec((8, tc), lambda b, t, j: (jnp.maximum((b * nt + t) * (tt // 8) - 1, 0), j)),
            pl.BlockSpec((tt, tc), lambda b, t, j: (b * nt + t, j)),
            pl.BlockSpec((kw, tc), lambda b, t, j: (0, j)),
        ],
        out_specs=pl.BlockSpec((tt, tc), lambda b, t, j: (b * nt + t, j)),
        out_shape=jax.ShapeDtypeStruct((m, conv_dim), BF16),
        scratch_shapes=[pltpu.VMEM((tt + 8, tc), F32)],
        compiler_params=_params("parallel", "parallel", "parallel"),
        name="gdn_qkv_prep",
    )(proj, proj, conv_w)


def _delta_kernel(q_ref, k_ref, v_ref, z_ref, bg_ref, grow_ref, onw_ref, o_ref,
                  s_ref, u_ref, wq_ref, qkkd_ref, *, heads, rep, chunks):
    @pl.when(pl.program_id(2) == 0)
    def _():
        s_ref[...] = jnp.zeros_like(s_ref)

    c_len, hd = GDN_CHUNK, GDN_HEAD
    row = lax.broadcasted_iota(jnp.int32, (c_len, c_len), 0)
    col = lax.broadcasted_iota(jnp.int32, (c_len, c_len), 1)
    tril, strict = col <= row, col < row
    eye = (row == col).astype(F32)
    onw = onw_ref[...]
    head_ids = range(heads)
    n_merged = c_len.bit_length() - 3

    def chunk_rows(c):
        start = c * c_len
        return pl.ds(start if isinstance(c, int) else pl.multiple_of(start, c_len), c_len)

    def prepare(c, carry):
        rows, slot = chunk_rows(c), c % 2
        kk, qk, kf, qf = [], [], [], []
        for kh in range(heads // rep):
            ksl = slice(kh * hd, (kh + 1) * hd)
            q_c, k_c = q_ref[rows, ksl], k_ref[rows, ksl]
            both = _dot_nt(jnp.concatenate([k_c, q_c], axis=0), k_c)
            kk += [both[:c_len]] * rep
            qk += [both[c_len:]] * rep
            kf += [k_c.astype(F32)] * rep
            qf += [q_c.astype(F32)] * rep
        bcol = [jnp.broadcast_to(bg_ref[rows, i:i + 1], (c_len, hd)) for i in head_ids]
        gcol = [jnp.broadcast_to(bg_ref[rows, heads + i:heads + i + 1], (c_len, hd)) for i in head_ids]
        grow = [grow_ref[c, i:i + 1, :] for i in head_ids]
        gam = [jnp.exp(jnp.where(tril, gcol[i][:, :c_len] - grow[i], -jnp.inf)) for i in head_ids]
        low = [jnp.where(strict, kk[i] * bcol[i][:, :c_len] * gam[i], 0.0) for i in head_ids]
        lb = [x.astype(BF16) for x in low]
        power = [_dot(lb[i], lb[i]) for i in head_ids]
        inv = [eye - low[i] for i in head_ids]
        for _ in range(n_merged):
            pb = [x.astype(BF16) for x in power]
            both = [_dot(jnp.concatenate([inv[i].astype(BF16), pb[i]], axis=0), pb[i]) for i in head_ids]
            inv = [inv[i] + both[i][:c_len] for i in head_ids]
            power = [both[i][c_len:] for i in head_ids]
        inv = [inv[i] + _dot(inv[i].astype(BF16), power[i].astype(BF16)) for i in head_ids]
        eg = [jnp.exp(gcol[i]) for i in head_ids]
        for i in head_ids:
            vsl = slice(i * hd, (i + 1) * hd)
            rhs = jnp.concatenate([v_ref[rows, vsl].astype(F32) * bcol[i], kf[i] * (bcol[i] * eg[i])], axis=1)
            uw = _dot(inv[i].astype(BF16), rhs.astype(BF16))
            u_ref[slot, i] = uw[:, :hd]
            wq_ref[slot, i] = jnp.concatenate([uw[:, hd:], qf[i] * eg[i]], axis=0).astype(BF16)
            k_dec = kf[i] * jnp.exp(grow[i][:, c_len - 1:c_len] - gcol[i])
            qkm = jnp.where(tril, qk[i] * gam[i], 0.0)
            qkkd_ref[slot, i] = jnp.concatenate([qkm, k_dec.T], axis=0).astype(BF16)
        return carry

    def advance(c, carry):
        rows, slot = chunk_rows(c), c % 2
        state = [s_ref[i] for i in head_ids]
        ws_qs = [_dot(wq_ref[slot, i], state[i].astype(BF16)) for i in head_ids]
        vb = [(u_ref[slot, i] - ws_qs[i][:c_len]).astype(BF16) for i in head_ids]
        upd = [_dot(qkkd_ref[slot, i], vb[i]) for i in head_ids]
        for i in head_ids:
            vsl = slice(i * hd, (i + 1) * hd)
            decay = jnp.exp(grow_ref[c, i:i + 1, c_len - 1:c_len])
            s_ref[i] = state[i] * decay + upd[i][c_len:]
            o = ws_qs[i][c_len:] + upd[i][:c_len]
            gate = _silu(z_ref[rows, vsl].astype(F32))
            o_ref[rows, vsl] = (_rms(o) * onw * gate).astype(o_ref.dtype)
        return carry

    def advance_and_prepare_next(c, carry):
        advance(c, carry)
        return prepare(c + 1, carry)

    prepare(0, 0)
    lax.fori_loop(0, chunks - 1, advance_and_prepare_next, 0)
    advance(chunks - 1, 0)


def _gdn_delta(qkv, proj, beta, gcum, o_norm_w, batch, seq, key_dim, val_dim):
    m = qkv.shape[0]
    hd, c_len = GDN_HEAD, GDN_CHUNK
    hk, hv = key_dim // hd, val_dim // hd
    rep = hv // hk
    g = min(GDN_HEADS_PER_STEP, hv)
    assert hv % g == 0 and g % rep == 0
    hg = hv // g
    cb = min(GDN_CHUNKS_PER_STEP, seq // c_len)
    tb = cb * c_len
    assert seq % tb == 0
    nt = seq // tb
    kw_ = g // rep * hd
    vw = g * hd
    bg = jnp.concatenate([beta.reshape(m, hg, g), gcum.reshape(m, hg, g)], axis=-1).transpose(1, 0, 2)
    grow = gcum.reshape(m // c_len, c_len, hv).transpose(0, 2, 1)
    kernel = functools.partial(_delta_kernel, heads=g, rep=rep, chunks=cb)
    return pl.pallas_call(
        kernel,
        grid=(batch, hg, nt),
        in_specs=[
            pl.BlockSpec((tb, kw_), lambda b, h, t: (b * nt + t, h)),
            pl.BlockSpec((tb, kw_), lambda b, h, t: (b * nt + t, key_dim // kw_ + h)),
            pl.BlockSpec((tb, vw), lambda b, h, t: (b * nt + t, 2 * key_dim // vw + h)),
            pl.BlockSpec((tb, vw), lambda b, h, t: (b * nt + t, (2 * key_dim + val_dim) // vw + h)),
            pl.BlockSpec((None, tb, 2 * g), lambda b, h, t: (h, b * nt + t, 0)),
            pl.BlockSpec((cb, g, c_len), lambda b, h, t: (b * nt + t, h, 0)),
            pl.BlockSpec((1, hd), lambda b, h, t: (0, 0)),
        ],
        out_specs=pl.BlockSpec((tb, vw), lambda b, h, t: (b * nt + t, h)),
        out_shape=jax.ShapeDtypeStruct((m, val_dim), BF16),
        scratch_shapes=[pltpu.VMEM((g, hd, hd), F32),
                        pltpu.VMEM((2, g, c_len, hd), F32),
                        pltpu.VMEM((2, g, 2 * c_len, hd), BF16),
                        pltpu.VMEM((2, g, c_len + hd, c_len), BF16)],
        compiler_params=_params("parallel", "parallel", "arbitrary"),
        name="gdn_delta",
    )(qkv, qkv, qkv, proj, bg, grow, o_norm_w.reshape(1, hd))


def _gdn_mixer(h, layer, w_in, w_in_bf, conv_w, a_log, dt_bias, o_norm_w, w_out_bf, batch, seq):
    conv_dim = conv_w.shape[1]
    hv = a_log.shape[0]
    val_dim = hv * GDN_HEAD
    key_dim = (conv_dim - val_dim) // 2
    n_main = conv_dim + val_dim
    proj = _matmul(h, w_in_bf, layer, n_main, 1024, 1024, BF16, "gdn_in_proj")
    beta, gcum = _gdn_gates(h, w_in, layer, n_main, a_log, dt_bias)
    qkv = _gdn_qkv_prep(proj, conv_w, batch, seq, key_dim)
    o = _gdn_delta(qkv, proj, beta, gcum, o_norm_w, batch, seq, key_dim, val_dim)
    return _matmul(o, w_out_bf, layer, w_out_bf.shape[-1], 512, 512, F32, "gdn_out_proj")


def _sgu_mix_kernel(u_ref, v_ref, z_ref, lnw_ref, lnb_ref, ws_ref, bst_ref, o_ref, wsm_ref, vn_ref, *, gd):
    groups, ck, _ = ws_ref.shape

    @pl.when(pl.program_id(0) == 0)
    def _():
        row = lax.broadcasted_iota(jnp.int32, (ck, ck), 0)
        col = lax.broadcasted_iota(jnp.int32, (ck, ck), 1)
        for g in range(groups):
            wsm_ref[g] = jnp.where(col <= row, ws_ref[g], 0.0).astype(wsm_ref.dtype)

    v = _gelu(v_ref[...].astype(F32))
    vc = v - jnp.mean(v, axis=-1, keepdims=True)
    inv = lax.rsqrt(jnp.mean(vc * vc, axis=-1, keepdims=True) + EPS)
    vn_ref[...] = (vc * inv * lnw_ref[...] + lnb_ref[...]).astype(vn_ref.dtype)
    for g in range(groups):
        sl = slice(g * gd, (g + 1) * gd)
        mixed = _dot(wsm_ref[g], vn_ref[:, sl]) + bst_ref[:, g:g + 1]
        u = _gelu(u_ref[:, sl].astype(F32))
        o_ref[:, sl] = (u * mixed * _silu(z_ref[:, sl].astype(F32))).astype(o_ref.dtype)


def _sgu_mix(proj, ln_w, ln_b, w_s, b_s):
    m = proj.shape[0]
    width = ln_w.shape[0]
    groups, ck, _ = w_s.shape
    gd = width // groups
    blk = lambda which: pl.BlockSpec((ck, width), lambda i: (i, which))
    vec = pl.BlockSpec((1, width), lambda i: (0, 0))
    return pl.pallas_call(
        functools.partial(_sgu_mix_kernel, gd=gd),
        grid=(m // ck,),
        in_specs=[blk(0), blk(1), blk(2), vec, vec,
                  pl.BlockSpec((groups, ck, ck), lambda i: (0, 0, 0)),
                  pl.BlockSpec((ck, groups), lambda i: (0, 0))],
        out_specs=pl.BlockSpec((ck, width), lambda i: (i, 0)),
        out_shape=jax.ShapeDtypeStruct((m, width), BF16),
        scratch_shapes=[pltpu.VMEM((groups, ck, ck), BF16), pltpu.VMEM((ck, width), BF16)],
        compiler_params=_params("arbitrary"),
        name="sgu_mix",
    )(proj, proj, proj, ln_w.reshape(1, width), ln_b.reshape(1, width), w_s, b_s.T)


def _sgu_mixer(h, layer, w_in_bf, ln_w, ln_b, w_s, b_s, w_out_bf):
    proj = _matmul(h, w_in_bf, layer, w_in_bf.shape[-1], 1024, 1024, BF16, "sgu_in_proj")
    s = _sgu_mix(proj, ln_w, ln_b, w_s, b_s)
    return _matmul(s, w_out_bf, layer, w_out_bf.shape[-1], 512, 512, F32, "sgu_out_proj")


def kernel(x, pre_norm, post_norm, gdn_w_in, gdn_conv_w, gdn_a_log, gdn_dt_bias, gdn_o_norm, gdn_w_out,
           sgu_w_in, sgu_ln_w, sgu_ln_b, sgu_w_s, sgu_b_s, sgu_w_out):
    batch, seq, d = x.shape
    depth = pre_norm.shape[0]
    xf = x.reshape(batch * seq, d)
    h = _prenorm(xf, pre_norm[0])
    gdn_w_in_bf, gdn_w_out_bf = gdn_w_in.astype(BF16), gdn_w_out.astype(BF16)
    sgu_w_in_bf, sgu_w_out_bf = sgu_w_in.astype(BF16), sgu_w_out.astype(BF16)
    for i in range(depth):
        j = i // 2
        if i % 2 == 0:
            y = _gdn_mixer(h, j, gdn_w_in, gdn_w_in_bf, gdn_conv_w[j], gdn_a_log[j], gdn_dt_bias[j],
                           gdn_o_norm[j], gdn_w_out_bf, batch, seq)
        else:
            y = _sgu_mixer(h, j, sgu_w_in_bf, sgu_ln_w[j], sgu_ln_b[j], sgu_w_s[j], sgu_b_s[j], sgu_w_out_bf)
        xf, h = _post_pre(y, xf, post_norm[i], pre_norm[i + 1] if i + 1 < depth else None)
    return xf.reshape(batch, seq, d)
```

```python
import functools
import math

import jax
import jax.numpy as jnp
from jax import lax
from jax.experimental import pallas as pl
from jax.experimental.pallas import tpu as pltpu

EPS = 1e-6
GDN_HEAD = 128
GDN_CHUNK = 64
GDN_HEADS_PER_STEP = 16
GDN_CHUNKS_PER_STEP = 16
VMEM_LIMIT_BYTES = 52 * 1024 * 1024
BF16 = jnp.bfloat16
F32 = jnp.float32


def _params(*semantics):
    return pltpu.CompilerParams(dimension_semantics=semantics, vmem_limit_bytes=VMEM_LIMIT_BYTES)


def _tile(n, want):
    t = want
    while n % t:
        t //= 2
    return t


def _dot(a, b):
    return jnp.dot(a, b, preferred_element_type=F32)


def _dot_nt(a, b):
    return lax.dot_general(a, b, (((1,), (1,)), ((), ())), preferred_element_type=F32)


def _rms(x):
    return x * lax.rsqrt(jnp.mean(x * x, axis=-1, keepdims=True) + EPS)


def _silu(x):
    return x * jax.nn.sigmoid(x)


def _gelu(x):
    return 0.5 * x * (1.0 + lax.erf(x * (1.0 / math.sqrt(2.0))))


def _prenorm_kernel(x_ref, w_ref, h_ref):
    h_ref[...] = (_rms(x_ref[...]) * w_ref[...]).astype(h_ref.dtype)


def _prenorm(x, w):
    m, d = x.shape
    tr = _tile(m, 256)
    return pl.pallas_call(
        _prenorm_kernel,
        grid=(m // tr,),
        in_specs=[pl.BlockSpec((tr, d), lambda i: (i, 0)), pl.BlockSpec((1, d), lambda i: (0, 0))],
        out_specs=pl.BlockSpec((tr, d), lambda i: (i, 0)),
        out_shape=jax.ShapeDtypeStruct((m, d), BF16),
        compiler_params=_params("parallel"),
        name="prenorm",
    )(x, w.reshape(1, d))


def _post_pre_kernel(y_ref, x_ref, pw_ref, nw_ref, xo_ref, h_ref):
    xn = x_ref[...] + _rms(y_ref[...].astype(F32)) * pw_ref[...]
    xo_ref[...] = xn
    h_ref[...] = (_rms(xn) * nw_ref[...]).astype(h_ref.dtype)


def _post_kernel(y_ref, x_ref, pw_ref, xo_ref):
    xo_ref[...] = x_ref[...] + _rms(y_ref[...].astype(F32)) * pw_ref[...]


def _post_pre(y, x, post_w, next_pre_w):
    m, d = x.shape
    tr = _tile(m, 256)
    row = pl.BlockSpec((tr, d), lambda i: (i, 0))
    vec = pl.BlockSpec((1, d), lambda i: (0, 0))
    if next_pre_w is None:
        return pl.pallas_call(
            _post_kernel, grid=(m // tr,), in_specs=[row, row, vec], out_specs=row,
            out_shape=jax.ShapeDtypeStruct((m, d), F32), compiler_params=_params("parallel"),
            name="postnorm",
        )(y, x, post_w.reshape(1, d)), None
    return pl.pallas_call(
        _post_pre_kernel, grid=(m // tr,), in_specs=[row, row, vec, vec], out_specs=[row, row],
        out_shape=[jax.ShapeDtypeStruct((m, d), F32), jax.ShapeDtypeStruct((m, d), BF16)],
        compiler_params=_params("parallel"), name="postnorm_prenorm",
    )(y, x, post_w.reshape(1, d), next_pre_w.reshape(1, d))


def _mm_kernel(a_ref, w_ref, o_ref):
    o_ref[...] = _dot(a_ref[...], w_ref[...]).astype(o_ref.dtype)


def _matmul(a, w_stack, layer, col0, n_cols, tm, tn, out_dtype, name):
    m, k = a.shape
    tm, tn = _tile(m, tm), _tile(n_cols, tn)
    assert col0 % tn == 0
    return pl.pallas_call(
        _mm_kernel,
        grid=(n_cols // tn, m // tm),
        in_specs=[pl.BlockSpec((tm, k), lambda j, i: (i, 0)),
                  pl.BlockSpec((None, k, tn), lambda j, i: (layer, 0, col0 // tn + j))],
        out_specs=pl.BlockSpec((tm, tn), lambda j, i: (i, j)),
        out_shape=jax.ShapeDtypeStruct((m, n_cols), out_dtype),
        compiler_params=_params("parallel", "parallel"),
        name=name,
    )(a, w_stack)


MM_EPILOGUE_COLS = 2 * GDN_HEAD


def _mm_conv_kernel(a_ref, w_ref, cw_ref, o_ref, tail_ref, ext_ref, *,
                    n_q_tiles, n_qk_tiles, tiles_per_seq, q_scale):
    j, i = pl.program_id(0), pl.program_id(1)
    tm, tn = o_ref.shape
    kw = cw_ref.shape[0]
    is_qk = j < n_qk_tiles
    scale = jnp.where(j < n_q_tiles, q_scale, 1.0)

    @pl.when(i % tiles_per_seq == 0)
    def _():
        tail_ref[...] = jnp.zeros_like(tail_ref)

    for c in range(tn // MM_EPILOGUE_COLS):
        sl = slice(c * MM_EPILOGUE_COLS, (c + 1) * MM_EPILOGUE_COLS)
        ext_ref[c, 8:, :] = _dot(a_ref[...], w_ref[:, sl])
        ext_ref[c, 0:8, :] = tail_ref[:, sl]
        tail_ref[:, sl] = ext_ref[c, tm:tm + 8, :]
        conv = cw_ref[kw - 1:kw, sl] * ext_ref[c, 8:, :]
        for tap in range(kw - 1):
            off = 8 - (kw - 1) + tap
            conv = conv + cw_ref[tap:tap + 1, sl] * ext_ref[c, off:off + tm, :]
        y = _silu(conv)
        for hh in range(MM_EPILOGUE_COLS // GDN_HEAD):
            blk = y[:, hh * GDN_HEAD:(hh + 1) * GDN_HEAD]
            inv = lax.rsqrt(jnp.sum(blk * blk, axis=-1, keepdims=True) + EPS) * scale
            col = c * MM_EPILOGUE_COLS + hh * GDN_HEAD
            o_ref[:, col:col + GDN_HEAD] = jnp.where(is_qk, blk * inv, blk).astype(o_ref.dtype)


def _gdn_qkv_proj(h, w_stack, layer, conv_w, seq, key_dim):
    m, k = h.shape
    kw, conv_dim = conv_w.shape
    assert kw - 1 <= 8
    tm, tn = _tile(seq, 1024), _tile(key_dim, 1024)
    assert tn % MM_EPILOGUE_COLS == 0
    kernel = functools.partial(_mm_conv_kernel, n_q_tiles=key_dim // tn, n_qk_tiles=2 * key_dim // tn,
                               tiles_per_seq=seq // tm, q_scale=GDN_HEAD ** -0.5)
    return pl.pallas_call(
        kernel,
        grid=(conv_dim // tn, m // tm),
        in_specs=[pl.BlockSpec((tm, k), lambda j, i: (i, 0)),
                  pl.BlockSpec((None, k, tn), lambda j, i: (layer, 0, j)),
                  pl.BlockSpec((kw, tn), lambda j, i: (0, j))],
        out_specs=pl.BlockSpec((tm, tn), lambda j, i: (i, j)),
        out_shape=jax.ShapeDtypeStruct((m, conv_dim), BF16),
        scratch_shapes=[pltpu.VMEM((8, tn), F32),
                        pltpu.VMEM((tn // MM_EPILOGUE_COLS, tm + 8, MM_EPILOGUE_COLS), F32)],
        compiler_params=_params("parallel", "arbitrary"),
        name="gdn_qkv_proj",
    )(h, w_stack, conv_w)


def _gates_kernel(h_ref, w_ref, alog_ref, dtb_ref, beta_ref, gc_ref):
    hv = alog_ref.shape[-1]
    h = h_ref[...]
    w = w_ref[...]
    w_hi = w.astype(BF16)
    w_lo = (w - w_hi.astype(F32)).astype(BF16)
    ba = _dot(h, w_hi) + _dot(h, w_lo)
    beta_ref[...] = jax.nn.sigmoid(ba[:, :hv])
    t = ba[:, hv:] + dtb_ref[...]
    softplus = jnp.maximum(t, 0.0) + jnp.log1p(jnp.exp(-jnp.abs(t)))
    g = -jnp.exp(alog_ref[...]) * softplus
    tm = h.shape[0]
    r = lax.broadcasted_iota(jnp.int32, (tm, tm), 0)
    c = lax.broadcasted_iota(jnp.int32, (tm, tm), 1)
    shift = GDN_CHUNK.bit_length() - 1
    tri = ((jnp.right_shift(r, shift) == jnp.right_shift(c, shift)) & (c <= r)).astype(F32)
    gc_ref[...] = jnp.dot(tri, g, precision=lax.Precision.HIGHEST, preferred_element_type=F32)


def _gdn_gates(h, w_in_stack, layer, col0, a_log, dt_bias):
    m, d = h.shape
    hv = a_log.shape[0]
    assert col0 % (2 * hv) == 0
    tm = _tile(m, 512)
    out = jax.ShapeDtypeStruct((m, hv), F32)
    return pl.pallas_call(
        _gates_kernel,
        grid=(m // tm,),
        in_specs=[pl.BlockSpec((tm, d), lambda i: (i, 0)),
                  pl.BlockSpec((None, d, 2 * hv), lambda i: (layer, 0, col0 // (2 * hv))),
                  pl.BlockSpec((1, hv), lambda i: (0, 0)), pl.BlockSpec((1, hv), lambda i: (0, 0))],
        out_specs=[pl.BlockSpec((tm, hv), lambda i: (i, 0))] * 2,
        out_shape=[out, out],
        compiler_params=_params("parallel"),
        name="gdn_gates",
    )(h, w_in_stack, a_log.reshape(1, hv), dt_bias.reshape(1, hv))


def _delta_kernel(q_ref, k_ref, v_ref, z_ref, bg_ref, grow_ref, onw_ref, o_ref,
                  s_ref, u_ref, wq_ref, qkkd_ref, *, heads, rep, chunks):
    @pl.when(pl.program_id(2) == 0)
    def _():
        s_ref[...] = jnp.zeros_like(s_ref)

    c_len, hd = GDN_CHUNK, GDN_HEAD
    row = lax.broadcasted_iota(jnp.int32, (c_len, c_len), 0)
    col = lax.broadcasted_iota(jnp.int32, (c_len, c_len), 1)
    tril, strict = col <= row, col < row
    eye = (row == col).astype(F32)
    onw = onw_ref[...]
    head_ids = range(heads)
    n_merged = c_len.bit_length() - 3

    def chunk_rows(c):
        start = c * c_len
        return pl.ds(start if isinstance(c, int) else pl.multiple_of(start, c_len), c_len)

    def prepare(c, carry):
        rows, slot = chunk_rows(c), c % 2
        kk, qk, kf, qf = [], [], [], []
        for kh in range(heads // rep):
            ksl = slice(kh * hd, (kh + 1) * hd)
            q_c, k_c = q_ref[rows, ksl], k_ref[rows, ksl]
            both = _dot_nt(jnp.concatenate([k_c, q_c], axis=0), k_c)
            kk += [both[:c_len]] * rep
            qk += [both[c_len:]] * rep
            kf += [k_c.astype(F32)] * rep
            qf += [q_c.astype(F32)] * rep
        bcol = [jnp.broadcast_to(bg_ref[rows, i:i + 1], (c_len, hd)) for i in head_ids]
        gcol = [jnp.broadcast_to(bg_ref[rows, heads + i:heads + i + 1], (c_len, hd)) for i in head_ids]
        grow = [grow_ref[c, i:i + 1, :] for i in head_ids]
        gam = [jnp.exp(jnp.where(tril, gcol[i][:, :c_len] - grow[i], -jnp.inf)) for i in head_ids]
        low = [jnp.where(strict, kk[i] * bcol[i][:, :c_len] * gam[i], 0.0) for i in head_ids]
        lb = [x.astype(BF16) for x in low]
        power = [_dot(lb[i], lb[i]) for i in head_ids]
        inv = [eye - low[i] for i in head_ids]
        for _ in range(n_merged):
            pb = [x.astype(BF16) for x in power]
            both = [_dot(jnp.concatenate([inv[i].astype(BF16), pb[i]], axis=0), pb[i]) for i in head_ids]
            inv = [inv[i] + both[i][:c_len] for i in head_ids]
            power = [both[i][c_len:] for i in head_ids]
        inv = [inv[i] + _dot(inv[i].astype(BF16), power[i].astype(BF16)) for i in head_ids]
        eg = [jnp.exp(gcol[i]) for i in head_ids]
        for i in head_ids:
            vsl = slice(i * hd, (i + 1) * hd)
            rhs = jnp.concatenate([v_ref[rows, vsl].astype(F32) * bcol[i], kf[i] * (bcol[i] * eg[i])], axis=1)
            uw = _dot(inv[i].astype(BF16), rhs.astype(BF16))
            u_ref[slot, i] = uw[:, :hd]
            wq_ref[slot, i] = jnp.concatenate([uw[:, hd:], qf[i] * eg[i]], axis=0).astype(BF16)
            k_dec = kf[i] * jnp.exp(grow[i][:, c_len - 1:c_len] - gcol[i])
            qkm = jnp.where(tril, qk[i] * gam[i], 0.0)
            qkkd_ref[slot, i] = jnp.concatenate([qkm, k_dec.T], axis=0).astype(BF16)
        return carry

    def advance(c, carry):
        rows, slot = chunk_rows(c), c % 2
        state = [s_ref[i] for i in head_ids]
        ws_qs = [_dot(wq_ref[slot, i], state[i].astype(BF16)) for i in head_ids]
        vb = [(u_ref[slot, i] - ws_qs[i][:c_len]).astype(BF16) for i in head_ids]
        upd = [_dot(qkkd_ref[slot, i], vb[i]) for i in head_ids]
        for i in head_ids:
            vsl = slice(i * hd, (i + 1) * hd)
            decay = jnp.exp(grow_ref[c, i:i + 1, c_len - 1:c_len])
            s_ref[i] = state[i] * decay + upd[i][c_len:]
            o = ws_qs[i][c_len:] + upd[i][:c_len]
            gate = _silu(z_ref[rows, vsl].astype(F32))
            o_ref[rows, vsl] = (_rms(o) * onw * gate).astype(o_ref.dtype)
        return carry

    def advance_and_prepare_next(c, carry):
        advance(c, carry)
        return prepare(c + 1, carry)

    prepare(0, 0)
    lax.fori_loop(0, chunks - 1, advance_and_prepare_next, 0)
    advance(chunks - 1, 0)


def _gdn_delta(qkv, z, beta, gcum, o_norm_w, batch, seq, key_dim, val_dim):
    m = qkv.shape[0]
    hd, c_len = GDN_HEAD, GDN_CHUNK
    hk, hv = key_dim // hd, val_dim // hd
    rep = hv // hk
    g = min(GDN_HEADS_PER_STEP, hv)
    assert hv % g == 0 and g % rep == 0
    hg = hv // g
    cb = min(GDN_CHUNKS_PER_STEP, seq // c_len)
    tb = cb * c_len
    assert seq % tb == 0
    nt = seq // tb
    kw_ = g // rep * hd
    vw = g * hd
    bg = jnp.concatenate([beta.reshape(m, hg, g), gcum.reshape(m, hg, g)], axis=-1).transpose(1, 0, 2)
    grow = gcum.reshape(m // c_len, c_len, hv).transpose(0, 2, 1)
    kernel = functools.partial(_delta_kernel, heads=g, rep=rep, chunks=cb)
    return pl.pallas_call(
        kernel,
        grid=(batch, hg, nt),
        in_specs=[
            pl.BlockSpec((tb, kw_), lambda b, h, t: (b * nt + t, h)),
            pl.BlockSpec((tb, kw_), lambda b, h, t: (b * nt + t, key_dim // kw_ + h)),
            pl.BlockSpec((tb, vw), lambda b, h, t: (b * nt + t, 2 * key_dim // vw + h)),
            pl.BlockSpec((tb, vw), lambda b, h, t: (b * nt + t, h)),
            pl.BlockSpec((None, tb, 2 * g), lambda b, h, t: (h, b * nt + t, 0)),
            pl.BlockSpec((cb, g, c_len), lambda b, h, t: (b * nt + t, h, 0)),
            pl.BlockSpec((1, hd), lambda b, h, t: (0, 0)),
        ],
        out_specs=pl.BlockSpec((tb, vw), lambda b, h, t: (b * nt + t, h)),
        out_shape=jax.ShapeDtypeStruct((m, val_dim), BF16),
        scratch_shapes=[pltpu.VMEM((g, hd, hd), F32),
                        pltpu.VMEM((2, g, c_len, hd), F32),
                        pltpu.VMEM((2, g, 2 * c_len, hd), BF16),
                        pltpu.VMEM((2, g, c_len + hd, c_len), BF16)],
        compiler_params=_params("parallel", "parallel", "arbitrary"),
        name="gdn_delta",
    )(qkv, qkv, qkv, z, bg, grow, o_norm_w.reshape(1, hd))


def _gdn_mixer(h, layer, w_in, w_in_bf, conv_w, a_log, dt_bias, o_norm_w, w_out_bf, batch, seq):
    conv_dim = conv_w.shape[1]
    hv = a_log.shape[0]
    val_dim = hv * GDN_HEAD
    key_dim = (conv_dim - val_dim) // 2
    qkv = _gdn_qkv_proj(h, w_in_bf, layer, conv_w, seq, key_dim)
    z = _matmul(h, w_in_bf, layer, conv_dim, val_dim, 1024, 1024, BF16, "gdn_z_proj")
    beta, gcum = _gdn_gates(h, w_in, layer, conv_dim + val_dim, a_log, dt_bias)
    o = _gdn_delta(qkv, z, beta, gcum, o_norm_w, batch, seq, key_dim, val_dim)
    return _matmul(o, w_out_bf, layer, 0, w_out_bf.shape[-1], 512, 512, BF16, "gdn_out_proj")


def _sgu_mix_kernel(u_ref, v_ref, z_ref, lnw_ref, lnb_ref, ws_ref, bst_ref, o_ref, wsm_ref, vn_ref, *, gd):
    groups, ck, _ = ws_ref.shape

    @pl.when(pl.program_id(0) == 0)
    def _():
        row = lax.broadcasted_iota(jnp.int32, (ck, ck), 0)
        col = lax.broadcasted_iota(jnp.int32, (ck, ck), 1)
        for g in range(groups):
            wsm_ref[g] = jnp.where(col <= row, ws_ref[g], 0.0).astype(wsm_ref.dtype)

    v = _gelu(v_ref[...].astype(F32))
    vc = v - jnp.mean(v, axis=-1, keepdims=True)
    inv = lax.rsqrt(jnp.mean(vc * vc, axis=-1, keepdims=True) + EPS)
    vn_ref[...] = (vc * inv * lnw_ref[...] + lnb_ref[...]).astype(vn_ref.dtype)
    for g in range(groups):
        sl = slice(g * gd, (g + 1) * gd)
        mixed = _dot(wsm_ref[g], vn_ref[:, sl]) + bst_ref[:, g:g + 1]
        u = _gelu(u_ref[:, sl].astype(F32))
        o_ref[:, sl] = (u * mixed * _silu(z_ref[:, sl].astype(F32))).astype(o_ref.dtype)


def _sgu_mix(proj, ln_w, ln_b, w_s, b_s):
    m = proj.shape[0]
    width = ln_w.shape[0]
    groups, ck, _ = w_s.shape
    gd = width // groups
    blk = lambda which: pl.BlockSpec((ck, width), lambda i: (i, which))
    vec = pl.BlockSpec((1, width), lambda i: (0, 0))
    return pl.pallas_call(
        functools.partial(_sgu_mix_kernel, gd=gd),
        grid=(m // ck,),
        in_specs=[blk(0), blk(1), blk(2), vec, vec,
                  pl.BlockSpec((groups, ck, ck), lambda i: (0, 0, 0)),
                  pl.BlockSpec((ck, groups), lambda i: (0, 0))],
        out_specs=pl.BlockSpec((ck, width), lambda i: (i, 0)),
        out_shape=jax.ShapeDtypeStruct((m, width), BF16),
        scratch_shapes=[pltpu.VMEM((groups, ck, ck), BF16), pltpu.VMEM((ck, width), BF16)],
        compiler_params=_params("arbitrary"),
        name="sgu_mix",
    )(proj, proj, proj, ln_w.reshape(1, width), ln_b.reshape(1, width), w_s, b_s.T)


def _sgu_mixer(h, layer, w_in_bf, ln_w, ln_b, w_s, b_s, w_out_bf):
    proj = _matmul(h, w_in_bf, layer, 0, w_in_bf.shape[-1], 1024, 1024, BF16, "sgu_in_proj")
    s = _sgu_mix(proj, ln_w, ln_b, w_s, b_s)
    return _matmul(s, w_out_bf, layer, 0, w_out_bf.shape[-1], 512, 512, BF16, "sgu_out_proj")


def kernel(x, pre_norm, post_norm, gdn_w_in, gdn_conv_w, gdn_a_log, gdn_dt_bias, gdn_o_norm, gdn_w_out,
           sgu_w_in, sgu_ln_w, sgu_ln_b, sgu_w_s, sgu_b_s, sgu_w_out):
    batch, seq, d = x.shape
    depth = pre_norm.shape[0]
    xf = x.reshape(batch * seq, d)
    h = _prenorm(xf, pre_norm[0])
    gdn_w_in_bf, gdn_w_out_bf = gdn_w_in.astype(BF16), gdn_w_out.astype(BF16)
    sgu_w_in_bf, sgu_w_out_bf = sgu_w_in.astype(BF16), sgu_w_out.astype(BF16)
    for i in range(depth):
        j = i // 2
        if i % 2 == 0:
            y = _gdn_mixer(h, j, gdn_w_in, gdn_w_in_bf, gdn_conv_w[j], gdn_a_log[j], gdn_dt_bias[j],
                           gdn_o_norm[j], gdn_w_out_bf, batch, seq)
        else:
            y = _sgu_mixer(h, j, sgu_w_in_bf, sgu_ln_w[j], sgu_ln_b[j], sgu_w_s[j], sgu_b_s[j], sgu_w_out_bf)
        xf, h = _post_pre(y, xf, post_norm[i], pre_norm[i + 1] if i + 1 < depth else None)
    return xf.reshape(batch, seq, d)
```
